```python
import math
import numpy as np
import jax
import jax.numpy as jnp
from jax import lax

D_MODEL = 1024
BATCH = 16
SEQ = 2048
DEPTH = 1

HEAD_DIM = 64
ROPE_DIM = HEAD_DIM // 4
ROPE_THETA = 500000.0
DIFF_HEADS = 4
DIFF_V_DIM = 2 * HEAD_DIM
NSA_HEADS = 8
NSA_KV_HEADS = 2
NSA_GROUP = NSA_HEADS // NSA_KV_HEADS
N_BRANCH = 3
CMP_BLOCK = 32
CMP_STRIDE = 16
CMP_HIDDEN = 4 * HEAD_DIM
SEL_BLOCK = 64
SEL_TOP_N = 16
WINDOW = 512
D_FF = 2816
Q_BLOCK = 128
SEL_Q_CHUNK = 64
MIX_WIDTH = DIFF_HEADS * DIFF_V_DIM + NSA_HEADS * HEAD_DIM
IN_SIZES = (DIFF_HEADS * 2 * HEAD_DIM, DIFF_HEADS * 2 * HEAD_DIM, DIFF_HEADS * DIFF_V_DIM,
            NSA_HEADS * HEAD_DIM) + (NSA_KV_HEADS * HEAD_DIM,) * 6 + (NSA_HEADS * N_BRANCH,)
IN_WIDTH = sum(IN_SIZES)
EPS = 1e-6
NEG_INF = -1e30
FORCE_SCORE = 1e9

kernel_name = "hymba_diffattn_nsa_macaron_layer"


def rms_norm(x, g):
    xf = x.astype(jnp.float32)
    y = xf * lax.rsqrt(jnp.mean(xf * xf, axis=-1, keepdims=True) + EPS)
    return (y * g.astype(jnp.float32)).astype(x.dtype)


def rope_tables(S):
    pos = jnp.arange(S, dtype=jnp.float32)
    inv = ROPE_THETA ** (-jnp.arange(0, ROPE_DIM, 2, dtype=jnp.float32) / ROPE_DIM)
    ang = pos[:, None] * inv[None, :]
    return jnp.cos(ang), jnp.sin(ang)


def partial_rope(x, cos, sin):
    xr = x[..., :ROPE_DIM].astype(jnp.float32)
    x1, x2 = xr[..., :ROPE_DIM // 2], xr[..., ROPE_DIM // 2:]
    rot = jnp.concatenate([x1 * cos - x2 * sin, x2 * cos + x1 * sin], axis=-1)
    return jnp.concatenate([rot.astype(x.dtype), x[..., ROPE_DIM:]], axis=-1)


def swiglu(h, w_gate, w_up, w_down):
    return (jax.nn.silu(h @ w_gate) * (h @ w_up)) @ w_down


def diff_attention(q, k, v, lam, subln_g, lambda_init):
    B, Hd, _, S, dh = q.shape
    nb = S // Q_BLOCK
    scale = dh ** -0.5
    qb = q.reshape(B, Hd, 2, nb, Q_BLOCK, dh).transpose(3, 0, 1, 2, 4, 5)
    kpos = jnp.arange(S)

    def block(args):
        qi, i = args
        s = jnp.einsum('bhmqd,bhmkd->bhmqk', qi, k).astype(jnp.float32) * scale
        qpos = i * Q_BLOCK + jnp.arange(Q_BLOCK)
        s = jnp.where(kpos[None, :] <= qpos[:, None], s, NEG_INF)
        p = jax.nn.softmax(s, axis=-1)
        a = p[:, :, 0] - lam * p[:, :, 1]
        return jnp.einsum('bhqk,bhkd->bhqd', a.astype(v.dtype), v)

    o = lax.map(block, (qb, jnp.arange(nb)))
    o = o.transpose(1, 2, 0, 3, 4).reshape(B, Hd, S, 2 * dh)
    o = rms_norm(o, subln_g) * (1.0 - lambda_init)
    return o.transpose(0, 2, 1, 3).reshape(B, S, Hd * 2 * dh)


def compress(x, pe, w1, w2):
    B, G, S, dh = x.shape
    r = CMP_BLOCK // CMP_STRIDE
    nc = S // CMP_STRIDE - r + 1
    sub = x.reshape(B, G, S // CMP_STRIDE, CMP_STRIDE, dh)
    blocks = jnp.concatenate([sub[:, :, j:j + nc] for j in range(r)], axis=3)
    blocks = (blocks + pe).reshape(B, G, nc, CMP_BLOCK * dh)
    return jax.nn.gelu(blocks @ w1) @ w2


def sel_overlap(nc, n_sel):
    cs = np.arange(nc)[:, None] * CMP_STRIDE
    ss = np.arange(n_sel)[None, :] * SEL_BLOCK
    ov = np.minimum(cs + CMP_BLOCK, ss + SEL_BLOCK) - np.maximum(cs, ss)
    return jnp.asarray(np.clip(ov, 0, None) / CMP_BLOCK, dtype=jnp.float32)


def nsa_attention(q, k_cmp, v_cmp, k_sel, v_sel, k_win, v_win, gates,
                  pe_k, wk1, wk2, pe_v, wv1, wv2):
    B, H, S, dh = q.shape
    G, Hg = NSA_KV_HEADS, NSA_GROUP
    scale = dh ** -0.5
    qg = q.reshape(B, G, Hg, S, dh)
    t = jnp.arange(S)

    kc = compress(k_cmp, pe_k, wk1, wk2)
    vc = compress(v_cmp, pe_v, wv1, wv2)
    nc = kc.shape[2]
    cend = jnp.arange(nc) * CMP_STRIDE + CMP_BLOCK - 1
    cvalid = cend[None, :] <= t[:, None]
    s = jnp.einsum('bghsd,bgcd->bghsc', qg, kc).astype(jnp.float32) * scale
    p_cmp = jnp.where(cvalid, jax.nn.softmax(jnp.where(cvalid, s, NEG_INF), axis=-1), 0.0)
    o_cmp = jnp.einsum('bghsc,bgcd->bghsd', p_cmp.astype(vc.dtype), vc)

    n_sel = S // SEL_BLOCK
    top_n = min(SEL_TOP_N, n_sel)
    imp = jnp.einsum('bghsc,cj->bgsj', p_cmp, sel_overlap(nc, n_sel))
    blk = jnp.arange(n_sel)[None, :]
    cur = (t // SEL_BLOCK)[:, None]
    causal_blk = blk * SEL_BLOCK <= t[:, None]
    forced = (blk == 0) | ((blk <= cur) & (blk >= cur - 1))
    imp = jnp.where(forced, FORCE_SCORE, jnp.where(causal_blk, imp, -1.0))
    _, idx = lax.top_k(imp, top_n)

    kb = k_sel.reshape(B, G, n_sel, SEL_BLOCK, dh)
    vb = v_sel.reshape(B, G, n_sel, SEL_BLOCK, dh)
    C = SEL_Q_CHUNK
    nq = S // C
    q_c = qg.reshape(B, G, Hg, nq, C, dh).transpose(3, 0, 1, 2, 4, 5)
    idx_c = idx.reshape(B, G, nq, C, top_n).transpose(2, 0, 1, 3, 4)
    bi = jnp.arange(B)[:, None, None, None]
    gi = jnp.arange(G)[None, :, None, None]

    def sel_chunk(args):
        qi, ii, c = args
        ks = kb[bi, gi, ii]
        vs = vb[bi, gi, ii]
        s = jnp.einsum('bghqd,bgqnld->bghqnl', qi, ks).astype(jnp.float32) * scale
        kpos = ii[..., None] * SEL_BLOCK + jnp.arange(SEL_BLOCK)
        qpos = c * C + jnp.arange(C)
        mask = kpos <= qpos[None, None, :, None, None]
        s = jnp.where(mask[:, :, None], s, NEG_INF).reshape(B, G, Hg, C, top_n * SEL_BLOCK)
        p = jax.nn.softmax(s, axis=-1).reshape(B, G, Hg, C, top_n, SEL_BLOCK)
        return jnp.einsum('bghqnl,bgqnld->bghqd', p.astype(vs.dtype), vs)

    o_sel = lax.map(sel_chunk, (q_c, idx_c, jnp.arange(nq)))
    o_sel = o_sel.transpose(1, 2, 3, 0, 4, 5).reshape(B, G, Hg, S, dh)

    nb = S // Q_BLOCK
    span = WINDOW + Q_BLOCK
    kp = jnp.pad(k_win, ((0, 0), (0, 0), (WINDOW, 0), (0, 0)))
    vp = jnp.pad(v_win, ((0, 0), (0, 0), (WINDOW, 0), (0, 0)))
    q_w = qg.reshape(B, G, Hg, nb, Q_BLOCK, dh).transpose(3, 0, 1, 2, 4, 5)

    def win_block(args):
        qi, i = args
        start = i * Q_BLOCK
        ks = lax.dynamic_slice_in_dim(kp, start, span, axis=2)
        vs = lax.dynamic_slice_in_dim(vp, start, span, axis=2)
        s = jnp.einsum('bghqd,bgkd->bghqk', qi, ks).astype(jnp.float32) * scale
        qpos = start + jnp.arange(Q_BLOCK)
        kpos = start - WINDOW + jnp.arange(span)
        d = qpos[:, None] - kpos[None, :]
        mask = (d >= 0) & (d < WINDOW) & (kpos[None, :] >= 0)
        p = jax.nn.softmax(jnp.where(mask, s, NEG_INF), axis=-1)
        return jnp.einsum('bghqk,bgkd->bghqd', p.astype(vs.dtype), vs)

    o_win = lax.map(win_block, (q_w, jnp.arange(nb)))
    o_win = o_win.transpose(1, 2, 3, 0, 4, 5).reshape(B, G, Hg, S, dh)

    o = gates[..., 0:1] * o_cmp + gates[..., 1:2] * o_sel + gates[..., 2:3] * o_win
    return o.transpose(0, 3, 1, 2, 4).reshape(B, S, H * dh)


def hybrid_mixer(h, w_in, lambda_q1, lambda_k1, lambda_q2, lambda_k2, diff_subln,
                 cmp_pe_k, cmp_k_w1, cmp_k_w2, cmp_pe_v, cmp_v_w1, cmp_v_w2, w_out, lambda_init):
    B, S, _ = h.shape
    dh = HEAD_DIM
    cos, sin = rope_tables(S)
    pts = [int(v) for v in np.cumsum(IN_SIZES)[:-1]]
    dq, dk, dv, nq, kc, vc, ks, vs, kw, vw, gl = jnp.split(h @ w_in, pts, axis=-1)

    dq = partial_rope(dq.reshape(B, S, DIFF_HEADS, 2, dh).transpose(0, 2, 3, 1, 4), cos, sin)
    dk = partial_rope(dk.reshape(B, S, DIFF_HEADS, 2, dh).transpose(0, 2, 3, 1, 4), cos, sin)
    dv = dv.reshape(B, S, DIFF_HEADS, DIFF_V_DIM).transpose(0, 2, 1, 3)
    lq1, lk1 = lambda_q1.astype(jnp.float32), lambda_k1.astype(jnp.float32)
    lq2, lk2 = lambda_q2.astype(jnp.float32), lambda_k2.astype(jnp.float32)
    lam = jnp.exp(jnp.sum(lq1 * lk1)) - jnp.exp(jnp.sum(lq2 * lk2)) + lambda_init
    o_diff = diff_attention(dq, dk, dv, lam, diff_subln, lambda_init)

    def kv_heads(a):
        return a.reshape(B, S, NSA_KV_HEADS, dh).transpose(0, 2, 1, 3)
    nq = partial_rope(nq.reshape(B, S, NSA_HEADS, dh).transpose(0, 2, 1, 3), cos, sin)
    gates = jax.nn.sigmoid(gl.reshape(B, S, NSA_HEADS, N_BRANCH)).transpose(0, 2, 1, 3)
    gates = gates.reshape(B, NSA_KV_HEADS, NSA_GROUP, S, N_BRANCH)
    o_nsa = nsa_attention(nq, kv_heads(kc), kv_heads(vc),
                          partial_rope(kv_heads(ks), cos, sin), kv_heads(vs),
                          partial_rope(kv_heads(kw), cos, sin), kv_heads(vw), gates,
                          cmp_pe_k, cmp_k_w1, cmp_k_w2, cmp_pe_v, cmp_v_w1, cmp_v_w2)

    return jnp.concatenate([o_diff, o_nsa], axis=-1) @ w_out


def setup_inputs(seed: int = 0) -> dict:
    key = jax.random.key(seed)
    ks = jax.random.split(key, 26)
    f32 = jnp.float32
    L = DEPTH

    def w(k, shape, fan_in):
        return jax.random.normal(k, shape, f32) * fan_in ** -0.5

    def gain(k, n):
        return 1.0 + 0.1 * jax.random.normal(k, (L, n), f32)

    return {
        'x': jax.random.normal(ks[0], (BATCH, SEQ, D_MODEL), f32),
        'ff1_norm_pre': gain(ks[1], D_MODEL),
        'ff1_w_gate': w(ks[2], (L, D_MODEL, D_FF), D_MODEL),
        'ff1_w_up': w(ks[3], (L, D_MODEL, D_FF), D_MODEL),
        'ff1_w_down': w(ks[4], (L, D_FF, D_MODEL), D_FF),
        'ff1_norm_post': gain(ks[5], D_MODEL),
        'mix_norm_pre': gain(ks[6], D_MODEL),
        'w_in': w(ks[7], (L, D_MODEL, IN_WIDTH), D_MODEL),
        'lambda_q1': 0.1 * jax.random.normal(ks[8], (L, HEAD_DIM), f32),
        'lambda_k1': 0.1 * jax.random.normal(ks[9], (L, HEAD_DIM), f32),
        'lambda_q2': 0.1 * jax.random.normal(ks[10], (L, HEAD_DIM), f32),
        'lambda_k2': 0.1 * jax.random.normal(ks[11], (L, HEAD_DIM), f32),
        'diff_subln': gain(ks[12], DIFF_V_DIM),
        'cmp_pe_k': 0.1 * jax.random.normal(ks[13], (L, CMP_BLOCK, HEAD_DIM), f32),
        'cmp_k_w1': w(ks[14], (L, CMP_BLOCK * HEAD_DIM, CMP_HIDDEN), CMP_BLOCK * HEAD_DIM),
        'cmp_k_w2': w(ks[15], (L, CMP_HIDDEN, HEAD_DIM), CMP_HIDDEN),
        'cmp_pe_v': 0.1 * jax.random.normal(ks[16], (L, CMP_BLOCK, HEAD_DIM), f32),
        'cmp_v_w1': w(ks[17], (L, CMP_BLOCK * HEAD_DIM, CMP_HIDDEN), CMP_BLOCK * HEAD_DIM),
        'cmp_v_w2': w(ks[18], (L, CMP_HIDDEN, HEAD_DIM), CMP_HIDDEN),
        'w_out': w(ks[19], (L, MIX_WIDTH, D_MODEL), MIX_WIDTH),
        'mix_norm_post': gain(ks[20], D_MODEL),
        'ff2_norm_pre': gain(ks[21], D_MODEL),
        'ff2_w_gate': w(ks[22], (L, D_MODEL, D_FF), D_MODEL),
        'ff2_w_up': w(ks[23], (L, D_MODEL, D_FF), D_MODEL),
        'ff2_w_down': w(ks[24], (L, D_FF, D_MODEL), D_FF),
        'ff2_norm_post': gain(ks[25], D_MODEL),
    }


def reference(x, ff1_norm_pre, ff1_w_gate, ff1_w_up, ff1_w_down, ff1_norm_post,
              mix_norm_pre, w_in, lambda_q1, lambda_k1, lambda_q2, lambda_k2, diff_subln,
              cmp_pe_k, cmp_k_w1, cmp_k_w2, cmp_pe_v, cmp_v_w1, cmp_v_w2, w_out, mix_norm_post,
              ff2_norm_pre, ff2_w_gate, ff2_w_up, ff2_w_down, ff2_norm_post):
    for l in range(DEPTH):
        lambda_init = 0.8 - 0.6 * math.exp(-0.3 * l)
        h = swiglu(rms_norm(x, ff1_norm_pre[l]), ff1_w_gate[l], ff1_w_up[l], ff1_w_down[l])
        x = x + 0.5 * rms_norm(h, ff1_norm_post[l])
        h = hybrid_mixer(rms_norm(x, mix_norm_pre[l]), w_in[l], lambda_q1[l], lambda_k1[l],
                         lambda_q2[l], lambda_k2[l], diff_subln[l], cmp_pe_k[l], cmp_k_w1[l],
                         cmp_k_w2[l], cmp_pe_v[l], cmp_v_w1[l], cmp_v_w2[l], w_out[l], lambda_init)
        x = x + rms_norm(h, mix_norm_post[l])
        h = swiglu(rms_norm(x, ff2_norm_pre[l]), ff2_w_gate[l], ff2_w_up[l], ff2_w_down[l])
        x = x + 0.5 * rms_norm(h, ff2_norm_post[l])
    return x
```

```python
import functools
import math

import numpy as np
import jax
import jax.numpy as jnp
from jax import lax
from jax.experimental import pallas as pl
from jax.experimental.pallas import tpu as pltpu

F32 = jnp.float32
BF16 = jnp.bfloat16

D_MODEL = 1024
HEAD_DIM = 64
ROPE_DIM = HEAD_DIM // 4
ROPE_THETA = 500000.0
DIFF_HEADS = 4
NSA_HEADS = 8
NSA_KV_HEADS = 2
NSA_GROUP = NSA_HEADS // NSA_KV_HEADS
N_BRANCH = 3
CMP_BLOCK = 32
CMP_STRIDE = 16
CMP_HIDDEN = 4 * HEAD_DIM
SEL_BLOCK = 64
SEL_SHIFT = SEL_BLOCK.bit_length() - 1
assert 1 << SEL_SHIFT == SEL_BLOCK
SEL_TOP_N = 16
WINDOW = 512
D_FF = 2816
EPS = 1e-6
NEG_INF = -1e30
FORCE_SCORE = 1e9
LAMBDA_INIT = 0.8 - 0.6 * math.exp(-0.3 * 0)
QK_SCALE = HEAD_DIM ** -0.5

LANES = 128
VMEM_LIMIT = 56 * 1024 * 1024

Y_WIDTH = 3072
CMP_OFF = Y_WIDTH
GATE_OFF = CMP_OFF + 256
W_IN_WIDTH = GATE_OFF + 256
ROPE_GROUPS = (0, 1, 3, 4)
SCALE_GROUPS = (0, 3)

FFN_TM = 512
FFN_FC = 256
PROJ_TM = 512
ATT_TQ = 256
assert WINDOW == 2 * ATT_TQ


def _rms(x, g):
    return x * lax.rsqrt(jnp.mean(x * x, axis=-1, keepdims=True) + EPS) * g


def _cparams(n_axes):
    return pltpu.CompilerParams(dimension_semantics=("arbitrary",) * n_axes,
                                vmem_limit_bytes=VMEM_LIMIT)


def _const_spec(shape):
    return pl.BlockSpec(shape, lambda *_: (0,) * len(shape), pipeline_mode=pl.Buffered(1))


def _ffn_kernel(x_ref, gpre_ref, wg_ref, wu_ref, wd_ref, gpost_ref, o_ref, h_ref, acc_ref):
    x = x_ref[...]
    h_ref[...] = _rms(x, gpre_ref[...]).astype(BF16)
    acc_ref[...] = jnp.zeros_like(acc_ref)

    def chunk(c, carry):
        h = h_ref[...]
        g = jnp.dot(h, wg_ref[c], preferred_element_type=F32)
        u = jnp.dot(h, wu_ref[c], preferred_element_type=F32)
        a = (g * jax.nn.sigmoid(g) * u).astype(BF16)
        acc_ref[...] += jnp.dot(a, wd_ref[c], preferred_element_type=F32)
        return carry

    lax.fori_loop(0, D_FF // FFN_FC, chunk, 0)
    o_ref[...] = x + 0.5 * _rms(acc_ref[...], gpost_ref[...])


def _ffn(x, g_pre, w_gate, w_up, w_down, g_post):
    t = x.shape[0]
    nc = D_FF // FFN_FC
    wg = w_gate.astype(BF16).reshape(D_MODEL, nc, FFN_FC).transpose(1, 0, 2)
    wu = w_up.astype(BF16).reshape(D_MODEL, nc, FFN_FC).transpose(1, 0, 2)
    wd = w_down.astype(BF16).reshape(nc, FFN_FC, D_MODEL)
    return pl.pallas_call(
        _ffn_kernel,
        out_shape=jax.ShapeDtypeStruct((t, D_MODEL), F32),
        grid=(t // FFN_TM,),
        in_specs=[
            pl.BlockSpec((FFN_TM, D_MODEL), lambda i: (i, 0)),
            _const_spec((1, D_MODEL)),
            _const_spec((nc, D_MODEL, FFN_FC)),
            _const_spec((nc, D_MODEL, FFN_FC)),
            _const_spec((nc, FFN_FC, D_MODEL)),
            _const_spec((1, D_MODEL)),
        ],
        out_specs=pl.BlockSpec((FFN_TM, D_MODEL), lambda i: (i, 0)),
        scratch_shapes=[pltpu.VMEM((FFN_TM, D_MODEL), BF16), pltpu.VMEM((FFN_TM, D_MODEL), F32)],
        compiler_params=_cparams(1),
        name="ffn",
    )(x, g_pre.reshape(1, -1), wg, wu, wd, g_post.reshape(1, -1))


def _w_in_columns():
    dq, dk, dv, nq, kc, vc, ks, vs, kw, vw, gl = np.cumsum(
        [0, 512, 512, 512, 512, 128, 128, 128, 128, 128, 128])
    cols = list(range(dq, kc))

    def dup(base, g):
        c = list(range(base + g * HEAD_DIM, base + (g + 1) * HEAD_DIM))
        return c + c

    for g in range(NSA_KV_HEADS):
        cols += dup(ks, g) + dup(kw, g)
    for g in range(NSA_KV_HEADS):
        cols += dup(vs, g) + dup(vw, g)
    for g in range(NSA_KV_HEADS):
        cols += list(range(kc + g * HEAD_DIM, kc + (g + 1) * HEAD_DIM))
        cols += list(range(vc + g * HEAD_DIM, vc + (g + 1) * HEAD_DIM))
    per_group = NSA_GROUP * N_BRANCH
    for g in range(NSA_KV_HEADS):
        cols += list(range(gl + g * per_group, gl + (g + 1) * per_group)) + [-1] * (LANES - per_group)
    assert len(cols) == W_IN_WIDTH
    return np.asarray(cols)


def _rope_tables(seq):
    pos = jnp.arange(seq, dtype=F32)
    inv = ROPE_THETA ** (-jnp.arange(0, ROPE_DIM, 2, dtype=F32) / ROPE_DIM)
    ang = pos[:, None] * inv[None, :]
    cos, sin = jnp.cos(ang), jnp.sin(ang)
    half = ROPE_DIM // 2
    lane = np.arange(LANES) % HEAD_DIM
    idx = lane % half
    c = jnp.where(lane < ROPE_DIM, cos[:, idx], 1.0)
    sa = jnp.where(lane < half, -sin[:, idx], 0.0)
    sb = jnp.where((lane >= half) & (lane < ROPE_DIM), sin[:, idx], 0.0)
    return jnp.concatenate([c, sa, sb], axis=1)


def _inproj_kernel(x_ref, g_ref, w_ref, rope_ref, y_ref, cmp_ref, gate_ref):
    h = _rms(x_ref[...], g_ref[...]).astype(BF16)
    cos = rope_ref[:, 0:LANES]
    sin_a = rope_ref[:, LANES:2 * LANES]
    sin_b = rope_ref[:, 2 * LANES:3 * LANES]
    half = ROPE_DIM // 2
    for grp in range(Y_WIDTH // 512):
        z = jnp.dot(h, w_ref[:, grp * 512:(grp + 1) * 512], preferred_element_type=F32)
        for s in range(4):
            zz = z[:, s * LANES:(s + 1) * LANES]
            if grp in ROPE_GROUPS:
                zz = zz * cos + pltpu.roll(zz, LANES - half, 1) * sin_a + pltpu.roll(zz, half, 1) * sin_b
            if grp in SCALE_GROUPS:
                zz = zz * QK_SCALE
            lo = grp * 512 + s * LANES
            y_ref[:, lo:lo + LANES] = zz.astype(BF16)
    cmp_ref[...] = jnp.dot(h, w_ref[:, CMP_OFF:CMP_OFF + 256], preferred_element_type=F32)
    gate_ref[...] = jax.nn.sigmoid(jnp.dot(h, w_ref[:, GATE_OFF:GATE_OFF + 256], preferred_element_type=F32))


def _inproj(x, g, w_in, seq):
    t = x.shape[0]
    cols = _w_in_columns()
    w = jnp.where(cols[None, :] >= 0, w_in[:, np.maximum(cols, 0)], 0.0).astype(BF16)
    rope = _rope_tables(seq)
    n_pos = seq // PROJ_TM
    return pl.pallas_call(
        _inproj_kernel,
        out_shape=(jax.ShapeDtypeStruct((t, Y_WIDTH), BF16),
                   jax.ShapeDtypeStruct((t, 256), F32),
                   jax.ShapeDtypeStruct((t, 256), F32)),
        grid=(t // PROJ_TM,),
        in_specs=[
            pl.BlockSpec((PROJ_TM, D_MODEL), lambda i: (i, 0)),
            _const_spec((1, D_MODEL)),
            _const_spec((D_MODEL, W_IN_WIDTH)),
            pl.BlockSpec((PROJ_TM, 3 * LANES), lambda i: (i % n_pos, 0)),
        ],
        out_specs=(pl.BlockSpec((PROJ_TM, Y_WIDTH), lambda i: (i, 0)),
                   pl.BlockSpec((PROJ_TM, 256), lambda i: (i, 0)),
                   pl.BlockSpec((PROJ_TM, 256), lambda i: (i, 0))),
        compiler_params=_cparams(1),
        name="inproj",
    )(x, g.reshape(1, -1), w, rope)


def _compress_kernel(x_ref, pe_ref, w1_ref, w2_ref, o_ref):
    x = x_ref[0, 0]
    half = CMP_STRIDE * HEAD_DIM
    xa = (x + pe_ref[0, 0:1, :]).astype(BF16)
    xb = (x + pe_ref[0, 1:2, :]).astype(BF16)
    p = jnp.dot(xa, w1_ref[0, 0:half, :], preferred_element_type=F32)
    q = jnp.dot(xb, w1_ref[0, half:2 * half, :], preferred_element_type=F32)
    n_sub = x.shape[0]
    h = p + pltpu.roll(q, n_sub - 1, 0)
    h = jax.nn.gelu(h, approximate=True)
    o_ref[0, 0] = jnp.dot(h.astype(BF16), w2_ref[0], preferred_element_type=F32).astype(BF16)


def _compress(cmp_raw, batch, seq, pe_k, w1_k, w2_k, pe_v, w1_v, w2_v):
    n_sub = seq // CMP_STRIDE
    flat = CMP_STRIDE * HEAD_DIM
    xr = cmp_raw.reshape(batch, n_sub, CMP_STRIDE, 2 * NSA_KV_HEADS, HEAD_DIM)
    xr = xr.transpose(0, 3, 1, 2, 4).reshape(batch, 2 * NSA_KV_HEADS, n_sub, flat)
    pe = jnp.stack([pe_k.reshape(2, flat), pe_v.reshape(2, flat)])
    w1 = jnp.stack([w1_k, w1_v]).astype(BF16)
    w2 = jnp.stack([w2_k, w2_v]).astype(BF16)
    w2 = jnp.concatenate([w2, w2], axis=-1)
    return pl.pallas_call(
        _compress_kernel,
        out_shape=jax.ShapeDtypeStruct((batch, 2 * NSA_KV_HEADS, n_sub, LANES), BF16),
        grid=(batch, 2 * NSA_KV_HEADS),
        in_specs=[
            pl.BlockSpec((1, 1, n_sub, flat), lambda b, j: (b, j, 0, 0)),
            pl.BlockSpec((1, 2, flat), lambda b, j: (j % 2, 0, 0)),
            pl.BlockSpec((1, CMP_BLOCK * HEAD_DIM, CMP_HIDDEN), lambda b, j: (j % 2, 0, 0)),
            pl.BlockSpec((1, CMP_HIDDEN, LANES), lambda b, j: (j % 2, 0, 0)),
        ],
        out_specs=pl.BlockSpec((1, 1, n_sub, LANES), lambda b, j: (b, j, 0, 0)),
        compiler_params=_cparams(2),
        name="compress",
    )(xr, pe, w1, w2)


def _scores(q, k):
    return lax.dot_general(q, k, (((1,), (1,)), ((), ())), preferred_element_type=F32)


def _online_step(s, v, m_ref, l_ref, acc_ref):
    m_old = m_ref[...]
    m_new = jnp.maximum(m_old, jnp.max(s, axis=-1, keepdims=True))
    alpha = jnp.exp(m_old - m_new)
    p = jnp.exp(s - m_new)
    l_ref[...] = alpha * l_ref[...] + jnp.sum(p, axis=-1, keepdims=True)
    acc_ref[...] = alpha * acc_ref[...] + jnp.dot(p.astype(BF16), v, preferred_element_type=F32)
    m_ref[...] = m_new


def _reset(m_ref, l_ref, acc_ref):
    m_ref[...] = jnp.full_like(m_ref, NEG_INF)
    l_ref[...] = jnp.zeros_like(l_ref)
    acc_ref[...] = jnp.zeros_like(acc_ref)


def _rows(ref, j):
    return ref[0, pl.ds(pl.multiple_of(j * ATT_TQ, ATT_TQ), ATT_TQ), :]


def _split_heads(pair):
    lane = lax.broadcasted_iota(jnp.int32, pair.shape, 1)
    zero = jnp.zeros_like(pair)
    return jnp.where(lane < HEAD_DIM, pair, zero), jnp.where(lane >= HEAD_DIM, pair, zero)


def _diff_kernel(q_ref, k_ref, v_ref, lam_ref, g_ref, o_ref, m_ref, l_ref, acc_ref):
    tq = ATT_TQ
    qi = pl.program_id(2)
    q1, q2 = _split_heads(q_ref[0])
    qq = jnp.concatenate([q1, q2], axis=0)
    _reset(m_ref, l_ref, acc_ref)

    def full_tile(j, carry):
        _online_step(_scores(qq, _rows(k_ref, j)), _rows(v_ref, j), m_ref, l_ref, acc_ref)
        return carry

    lax.fori_loop(0, qi, full_tile, 0)
    row = lax.broadcasted_iota(jnp.int32, (2 * tq, tq), 0) & (tq - 1)
    col = lax.broadcasted_iota(jnp.int32, (2 * tq, tq), 1)
    s = jnp.where(col <= row, _scores(qq, _rows(k_ref, qi)), NEG_INF)
    _online_step(s, _rows(v_ref, qi), m_ref, l_ref, acc_ref)

    lp = lam_ref[...]
    lam = (jnp.exp(jnp.sum(lp[0:1] * lp[1:2], axis=-1, keepdims=True))
           - jnp.exp(jnp.sum(lp[2:3] * lp[3:4], axis=-1, keepdims=True)) + LAMBDA_INIT)
    o = acc_ref[...] / l_ref[...]
    o = o[0:tq] - lam * o[tq:2 * tq]
    o_ref[0] = (_rms(o, g_ref[...]) * (1.0 - LAMBDA_INIT)).astype(BF16)


def _diff_attention(y, lam_params, subln, batch, seq):
    tq = ATT_TQ
    return pl.pallas_call(
        _diff_kernel,
        out_shape=jax.ShapeDtypeStruct((batch, seq, DIFF_HEADS * 2 * HEAD_DIM), BF16),
        grid=(batch, DIFF_HEADS, seq // tq),
        in_specs=[
            pl.BlockSpec((1, tq, LANES), lambda b, h, i: (b, i, h)),
            pl.BlockSpec((1, seq, LANES), lambda b, h, i: (b, 0, DIFF_HEADS + h)),
            pl.BlockSpec((1, seq, LANES), lambda b, h, i: (b, 0, 2 * DIFF_HEADS + h)),
            pl.BlockSpec((4, HEAD_DIM), lambda b, h, i: (0, 0)),
            pl.BlockSpec((1, LANES), lambda b, h, i: (0, 0)),
        ],
        out_specs=pl.BlockSpec((1, tq, LANES), lambda b, h, i: (b, i, h)),
        scratch_shapes=[pltpu.VMEM((2 * tq, 1), F32), pltpu.VMEM((2 * tq, 1), F32),
                        pltpu.VMEM((2 * tq, LANES), F32)],
        compiler_params=_cparams(3),
        name="diffattn",
    )(y, y, y, lam_params, subln.reshape(1, -1))


def _block_onehot(seq):
    e = (np.arange(seq)[:, None] // SEL_BLOCK) == np.arange(LANES)[None, :]
    return jnp.asarray(e, dtype=BF16)


def _overlap_t(seq):
    n_sel = seq // SEL_BLOCK
    cs = np.arange(LANES)[None, :] * CMP_STRIDE
    ss = np.arange(n_sel)[:, None] * SEL_BLOCK
    ov = np.minimum(cs + CMP_BLOCK, ss + SEL_BLOCK) - np.maximum(cs, ss)
    ov = np.clip(ov, 0, None) / CMP_BLOCK
    ov[:, seq // CMP_STRIDE - CMP_BLOCK // CMP_STRIDE + 1:] = 0.0
    return jnp.asarray(ov, dtype=BF16)


def _nsa_kernel(q_ref, ks_ref, kw_ref, vs_ref, vw_ref, kc_ref, vc_ref, gate_ref, e_ref, ov_ref, o_ref,
                m_s, l_s, acc_s, m_w, l_w, acc_w):
    tq = ATT_TQ
    hg = NSA_GROUP
    n_sel = ov_ref.shape[0]
    qi = pl.program_id(2)
    q = q_ref[0]
    heads = _split_heads(q[:, 0:LANES]) + _split_heads(q[:, LANES:2 * LANES])
    q4 = jnp.concatenate(heads, axis=0)
    row = lax.broadcasted_iota(jnp.int32, (hg * tq, tq), 0) & (tq - 1)
    col = lax.broadcasted_iota(jnp.int32, (hg * tq, tq), 1)

    n_cmp = kc_ref.shape[2]
    t_row = qi * tq + (lax.broadcasted_iota(jnp.int32, (hg * tq, n_cmp), 0) & (tq - 1))
    c_end = lax.broadcasted_iota(jnp.int32, (hg * tq, n_cmp), 1) * CMP_STRIDE + (CMP_BLOCK - 1)
    c_valid = c_end <= t_row
    s_c = jnp.where(c_valid, _scores(q4, kc_ref[0, 0]), NEG_INF)
    p_c = jnp.exp(s_c - jnp.max(s_c, axis=-1, keepdims=True))
    p_c = jnp.where(c_valid, p_c / jnp.sum(p_c, axis=-1, keepdims=True), 0.0)
    o_cmp = jnp.dot(p_c.astype(BF16), vc_ref[0, 0], preferred_element_type=F32)

    p_sum = p_c[0:tq]
    for h in range(1, hg):
        p_sum = p_sum + p_c[h * tq:(h + 1) * tq]
    p_hi = p_sum.astype(BF16)
    p_lo = (p_sum - p_hi.astype(F32)).astype(BF16)
    ov = ov_ref[...]
    imp = _scores(ov, p_hi) + _scores(ov, p_lo)
    blk = lax.broadcasted_iota(jnp.int32, (n_sel, tq), 0)
    cur = jnp.right_shift(qi * tq + lax.broadcasted_iota(jnp.int32, (n_sel, tq), 1), SEL_SHIFT)
    forced = (blk == 0) | ((blk <= cur) & (blk >= cur - 1))
    imp = jnp.where(forced, FORCE_SCORE, jnp.where(blk <= cur, imp, -1.0))
    rank = jnp.zeros((n_sel, tq), F32)
    for k in range(n_sel):
        r = imp[k:k + 1, :]
        rank = rank + jnp.where(blk > k, jnp.where(r >= imp, 1.0, 0.0), jnp.where(r > imp, 1.0, 0.0))
    pen = jnp.where(rank < min(SEL_TOP_N, n_sel), 0.0, NEG_INF)
    pen = jnp.concatenate([pen, jnp.zeros((LANES - n_sel, tq), F32)], axis=0)
    pen = jnp.transpose(pen).astype(BF16)
    q4e = jnp.concatenate([q4, jnp.concatenate([pen] * hg, axis=0)], axis=1)

    _reset(m_s, l_s, acc_s)

    def sel_scores(j):
        onehot = e_ref[pl.ds(pl.multiple_of(j * tq, tq), tq), :]
        return _scores(q4e, jnp.concatenate([_rows(ks_ref, j), onehot], axis=1))

    def sel_tile(j, carry):
        _online_step(sel_scores(j), _rows(vs_ref, j), m_s, l_s, acc_s)
        return carry

    lax.fori_loop(0, qi, sel_tile, 0)
    _online_step(jnp.where(col <= row, sel_scores(qi), NEG_INF), _rows(vs_ref, qi), m_s, l_s, acc_s)

    _reset(m_w, l_w, acc_w)

    @pl.when(qi >= 2)
    def _():
        s = jnp.where(col > row, _scores(q4, _rows(kw_ref, qi - 2)), NEG_INF)
        _online_step(s, _rows(vw_ref, qi - 2), m_w, l_w, acc_w)

    @pl.when(qi >= 1)
    def _():
        _online_step(_scores(q4, _rows(kw_ref, qi - 1)), _rows(vw_ref, qi - 1), m_w, l_w, acc_w)

    s = jnp.where(col <= row, _scores(q4, _rows(kw_ref, qi)), NEG_INF)
    _online_step(s, _rows(vw_ref, qi), m_w, l_w, acc_w)

    o_sel = acc_s[...] / l_s[...]
    o_win = acc_w[...] / l_w[...]
    gate = gate_ref[0]
    outs = []
    for h in range(hg):
        rs = slice(h * tq, (h + 1) * tq)
        c = h * N_BRANCH
        outs.append(gate[:, c:c + 1] * o_cmp[rs] + gate[:, c + 1:c + 2] * o_sel[rs]
                    + gate[:, c + 2:c + 3] * o_win[rs])
    lane = lax.broadcasted_iota(jnp.int32, (tq, LANES), 1)
    pairs = [jnp.where(lane < HEAD_DIM, outs[2 * j], outs[2 * j + 1]) for j in range(hg // 2)]
    o_ref[0] = jnp.concatenate(pairs, axis=1).astype(BF16)


def _nsa_attention(y, kcvc, gates, batch, seq):
    tq = ATT_TQ
    hg = NSA_GROUP
    n_sub = seq // CMP_STRIDE
    n_sel = seq // SEL_BLOCK
    assert n_sub == LANES and n_sel <= LANES
    kv0 = 16
    return pl.pallas_call(
        _nsa_kernel,
        out_shape=jax.ShapeDtypeStruct((batch, seq, NSA_HEADS * HEAD_DIM), BF16),
        grid=(batch, NSA_KV_HEADS, seq // tq),
        in_specs=[
            pl.BlockSpec((1, tq, 2 * LANES), lambda b, g, i: (b, i, 6 + g)),
            pl.BlockSpec((1, seq, LANES), lambda b, g, i: (b, 0, kv0 + 2 * g)),
            pl.BlockSpec((1, seq, LANES), lambda b, g, i: (b, 0, kv0 + 2 * g + 1)),
            pl.BlockSpec((1, seq, LANES), lambda b, g, i: (b, 0, kv0 + 4 + 2 * g)),
            pl.BlockSpec((1, seq, LANES), lambda b, g, i: (b, 0, kv0 + 4 + 2 * g + 1)),
            pl.BlockSpec((1, 1, n_sub, LANES), lambda b, g, i: (b, 2 * g, 0, 0)),
            pl.BlockSpec((1, 1, n_sub, LANES), lambda b, g, i: (b, 2 * g + 1, 0, 0)),
            pl.BlockSpec((1, tq, LANES), lambda b, g, i: (b, i, g)),
            pl.BlockSpec((seq, LANES), lambda b, g, i: (0, 0)),
            pl.BlockSpec((n_sel, LANES), lambda b, g, i: (0, 0)),
        ],
        out_specs=pl.BlockSpec((1, tq, 2 * LANES), lambda b, g, i: (b, i, g)),
        scratch_shapes=[pltpu.VMEM((hg * tq, 1), F32), pltpu.VMEM((hg * tq, 1), F32),
                        pltpu.VMEM((hg * tq, LANES), F32)] * 2,
        compiler_params=_cparams(3),
        name="nsa",
    )(y, y, y, y, y, kcvc, kcvc, gates, _block_onehot(seq), _overlap_t(seq))


def _outproj_kernel(od_ref, on_ref, w_ref, g_ref, x_ref, o_ref):
    half = od_ref.shape[1]
    h = jnp.dot(od_ref[...], w_ref[0:half, :], preferred_element_type=F32)
    h = h + jnp.dot(on_ref[...], w_ref[half:2 * half, :], preferred_element_type=F32)
    o_ref[...] = x_ref[...] + _rms(h, g_ref[...])


def _outproj(o_diff, o_nsa, w_out, g, x):
    t = x.shape[0]
    half = o_diff.shape[1]
    return pl.pallas_call(
        _outproj_kernel,
        out_shape=jax.ShapeDtypeStruct((t, D_MODEL), F32),
        grid=(t // PROJ_TM,),
        in_specs=[
            pl.BlockSpec((PROJ_TM, half), lambda i: (i, 0)),
            pl.BlockSpec((PROJ_TM, half), lambda i: (i, 0)),
            _const_spec((2 * half, D_MODEL)),
            _const_spec((1, D_MODEL)),
            pl.BlockSpec((PROJ_TM, D_MODEL), lambda i: (i, 0)),
        ],
        out_specs=pl.BlockSpec((PROJ_TM, D_MODEL), lambda i: (i, 0)),
        compiler_params=_cparams(1),
        name="outproj",
    )(o_diff, o_nsa, w_out.astype(BF16), g.reshape(1, -1), x)


def kernel(x, ff1_norm_pre, ff1_w_gate, ff1_w_up, ff1_w_down, ff1_norm_post, mix_norm_pre, w_in, lambda_q1, lambda_k1, lambda_q2, lambda_k2, diff_subln, cmp_pe_k, cmp_k_w1, cmp_k_w2, cmp_pe_v, cmp_v_w1, cmp_v_w2, w_out, mix_norm_post, ff2_norm_pre, ff2_w_gate, ff2_w_up, ff2_w_down, ff2_norm_post):
    batch, seq, d = x.shape
    assert d == D_MODEL and ff1_w_gate.shape[0] == 1, "single-layer kernel"
    t = batch * seq
    xf = x.reshape(t, d)

    x1 = _ffn(xf, ff1_norm_pre[0], ff1_w_gate[0], ff1_w_up[0], ff1_w_down[0], ff1_norm_post[0])

    y, cmp_raw, gates = _inproj(x1, mix_norm_pre[0], w_in[0], seq)
    kcvc = _compress(cmp_raw, batch, seq, cmp_pe_k[0], cmp_k_w1[0], cmp_k_w2[0],
                     cmp_pe_v[0], cmp_v_w1[0], cmp_v_w2[0])
    y = y.reshape(batch, seq, Y_WIDTH)
    lam_params = jnp.stack([lambda_q1[0], lambda_k1[0], lambda_q2[0], lambda_k2[0]]).astype(F32)
    o_diff = _diff_attention(y, lam_params, diff_subln[0], batch, seq)
    o_nsa = _nsa_attention(y, kcvc, gates.reshape(batch, seq, 256), batch, seq)

    x2 = _outproj(o_diff.reshape(t, -1), o_nsa.reshape(t, -1), w_out[0], mix_norm_post[0], x1)
    out = _ffn(x2, ff2_norm_pre[0], ff2_w_gate[0], ff2_w_up[0], ff2_w_down[0], ff2_norm_post[0])
    return out.reshape(batch, seq, d)
```

```python
import functools
import math

import numpy as np
import jax
import jax.numpy as jnp
from jax import lax
from jax.experimental import pallas as pl
from jax.experimental.pallas import tpu as pltpu

F32 = jnp.float32
BF16 = jnp.bfloat16

D_MODEL = 1024
HEAD_DIM = 64
ROPE_DIM = HEAD_DIM // 4
ROPE_THETA = 500000.0
DIFF_HEADS = 4
NSA_HEADS = 8
NSA_KV_HEADS = 2
NSA_GROUP = NSA_HEADS // NSA_KV_HEADS
N_BRANCH = 3
CMP_BLOCK = 32
CMP_STRIDE = 16
CMP_HIDDEN = 4 * HEAD_DIM
SEL_BLOCK = 64
SEL_SHIFT = SEL_BLOCK.bit_length() - 1
assert 1 << SEL_SHIFT == SEL_BLOCK
SEL_TOP_N = 16
WINDOW = 512
D_FF = 2816
EPS = 1e-6
NEG_INF = -1e30
FORCE_SCORE = 1e9
LAMBDA_INIT = 0.8 - 0.6 * math.exp(-0.3 * 0)
QK_SCALE = HEAD_DIM ** -0.5

LANES = 128
VMEM_LIMIT = 56 * 1024 * 1024

Y_WIDTH = 3072
CMP_OFF = Y_WIDTH
GATE_OFF = CMP_OFF + 256
W_IN_WIDTH = GATE_OFF + 256
ROPE_GROUPS = (0, 1, 3, 4)
SCALE_GROUPS = (0, 3)

FFN_TM = 512
FFN_FC = 256
PROJ_TM = 512
ATT_TQ = 256
assert WINDOW == 2 * ATT_TQ
DIFF_TQ = 256


def _rms(x, g):
    return x * lax.rsqrt(jnp.mean(x * x, axis=-1, keepdims=True) + EPS) * g


def _cparams(n_axes):
    return pltpu.CompilerParams(dimension_semantics=("arbitrary",) * n_axes,
                                vmem_limit_bytes=VMEM_LIMIT)


def _const_spec(shape):
    return pl.BlockSpec(shape, lambda *_: (0,) * len(shape), pipeline_mode=pl.Buffered(1))


def _ffn_kernel(x_ref, gpre_ref, wg_ref, wu_ref, wd_ref, gpost_ref, o_ref, h_ref, acc_ref):
    x = x_ref[...]
    h_ref[...] = _rms(x, gpre_ref[...]).astype(BF16)
    acc_ref[...] = jnp.zeros_like(acc_ref)

    def chunk(c, carry):
        h = h_ref[...]
        g = jnp.dot(h, wg_ref[c], preferred_element_type=F32)
        u = jnp.dot(h, wu_ref[c], preferred_element_type=F32)
        a = (g * jax.nn.sigmoid(g) * u).astype(BF16)
        acc_ref[...] += jnp.dot(a, wd_ref[c], preferred_element_type=F32)
        return carry

    lax.fori_loop(0, D_FF // FFN_FC, chunk, 0)
    o_ref[...] = x + 0.5 * _rms(acc_ref[...], gpost_ref[...])


def _ffn(x, g_pre, w_gate, w_up, w_down, g_post):
    t = x.shape[0]
    nc = D_FF // FFN_FC
    wg = w_gate.astype(BF16).reshape(D_MODEL, nc, FFN_FC).transpose(1, 0, 2)
    wu = w_up.astype(BF16).reshape(D_MODEL, nc, FFN_FC).transpose(1, 0, 2)
    wd = w_down.astype(BF16).reshape(nc, FFN_FC, D_MODEL)
    return pl.pallas_call(
        _ffn_kernel,
        out_shape=jax.ShapeDtypeStruct((t, D_MODEL), F32),
        grid=(t // FFN_TM,),
        in_specs=[
            pl.BlockSpec((FFN_TM, D_MODEL), lambda i: (i, 0)),
            _const_spec((1, D_MODEL)),
            _const_spec((nc, D_MODEL, FFN_FC)),
            _const_spec((nc, D_MODEL, FFN_FC)),
            _const_spec((nc, FFN_FC, D_MODEL)),
            _const_spec((1, D_MODEL)),
        ],
        out_specs=pl.BlockSpec((FFN_TM, D_MODEL), lambda i: (i, 0)),
        scratch_shapes=[pltpu.VMEM((FFN_TM, D_MODEL), BF16), pltpu.VMEM((FFN_TM, D_MODEL), F32)],
        compiler_params=_cparams(1),
        name="ffn",
    )(x, g_pre.reshape(1, -1), wg, wu, wd, g_post.reshape(1, -1))


def _w_in_columns():
    dq, dk, dv, nq, kc, vc, ks, vs, kw, vw, gl = np.cumsum(
        [0, 512, 512, 512, 512, 128, 128, 128, 128, 128, 128])
    cols = list(range(dq, kc))

    def dup(base, g):
        c = list(range(base + g * HEAD_DIM, base + (g + 1) * HEAD_DIM))
        return c + c

    for g in range(NSA_KV_HEADS):
        cols += dup(ks, g) + dup(kw, g)
    for g in range(NSA_KV_HEADS):
        cols += dup(vs, g) + dup(vw, g)
    for g in range(NSA_KV_HEADS):
        cols += list(range(kc + g * HEAD_DIM, kc + (g + 1) * HEAD_DIM))
        cols += list(range(vc + g * HEAD_DIM, vc + (g + 1) * HEAD_DIM))
    per_group = NSA_GROUP * N_BRANCH
    for g in range(NSA_KV_HEADS):
        cols += list(range(gl + g * per_group, gl + (g + 1) * per_group)) + [-1] * (LANES - per_group)
    assert len(cols) == W_IN_WIDTH
    return np.asarray(cols)


def _rope_tables(seq):
    pos = jnp.arange(seq, dtype=F32)
    inv = ROPE_THETA ** (-jnp.arange(0, ROPE_DIM, 2, dtype=F32) / ROPE_DIM)
    ang = pos[:, None] * inv[None, :]
    cos, sin = jnp.cos(ang), jnp.sin(ang)
    half = ROPE_DIM // 2
    lane = np.arange(LANES) % HEAD_DIM
    idx = lane % half
    c = jnp.where(lane < ROPE_DIM, cos[:, idx], 1.0)
    sa = jnp.where(lane < half, -sin[:, idx], 0.0)
    sb = jnp.where((lane >= half) & (lane < ROPE_DIM), sin[:, idx], 0.0)
    return jnp.concatenate([c, sa, sb], axis=1)


def _inproj_kernel(x_ref, g_ref, w_ref, rope_ref, y_ref, cmp_ref, gate_ref):
    h = _rms(x_ref[...], g_ref[...]).astype(BF16)
    cos = rope_ref[:, 0:LANES]
    sin_a = rope_ref[:, LANES:2 * LANES]
    sin_b = rope_ref[:, 2 * LANES:3 * LANES]
    half = ROPE_DIM // 2
    for grp in range(Y_WIDTH // 512):
        z = jnp.dot(h, w_ref[:, grp * 512:(grp + 1) * 512], preferred_element_type=F32)
        for s in range(4):
            zz = z[:, s * LANES:(s + 1) * LANES]
            if grp in ROPE_GROUPS:
                zz = zz * cos + pltpu.roll(zz, LANES - half, 1) * sin_a + pltpu.roll(zz, half, 1) * sin_b
            if grp in SCALE_GROUPS:
                zz = zz * QK_SCALE
            lo = grp * 512 + s * LANES
            y_ref[:, lo:lo + LANES] = zz.astype(BF16)
    cmp_ref[...] = jnp.dot(h, w_ref[:, CMP_OFF:CMP_OFF + 256], preferred_element_type=F32)
    gate_ref[...] = jax.nn.sigmoid(jnp.dot(h, w_ref[:, GATE_OFF:GATE_OFF + 256], preferred_element_type=F32))


def _inproj(x, g, w_in, seq):
    t = x.shape[0]
    cols = _w_in_columns()
    w = jnp.where(cols[None, :] >= 0, w_in[:, np.maximum(cols, 0)], 0.0).astype(BF16)
    rope = _rope_tables(seq)
    n_pos = seq // PROJ_TM
    return pl.pallas_call(
        _inproj_kernel,
        out_shape=(jax.ShapeDtypeStruct((t, Y_WIDTH), BF16),
                   jax.ShapeDtypeStruct((t, 256), F32),
                   jax.ShapeDtypeStruct((t, 256), F32)),
        grid=(t // PROJ_TM,),
        in_specs=[
            pl.BlockSpec((PROJ_TM, D_MODEL), lambda i: (i, 0)),
            _const_spec((1, D_MODEL)),
            _const_spec((D_MODEL, W_IN_WIDTH)),
            pl.BlockSpec((PROJ_TM, 3 * LANES), lambda i: (i % n_pos, 0)),
        ],
        out_specs=(pl.BlockSpec((PROJ_TM, Y_WIDTH), lambda i: (i, 0)),
                   pl.BlockSpec((PROJ_TM, 256), lambda i: (i, 0)),
                   pl.BlockSpec((PROJ_TM, 256), lambda i: (i, 0))),
        compiler_params=_cparams(1),
        name="inproj",
    )(x, g.reshape(1, -1), w, rope)


def _compress_kernel(x_ref, pe_ref, w1_ref, w2_ref, o_ref):
    x = x_ref[0, 0]
    half = CMP_STRIDE * HEAD_DIM
    xa = (x + pe_ref[0, 0:1, :]).astype(BF16)
    xb = (x + pe_ref[0, 1:2, :]).astype(BF16)
    p = jnp.dot(xa, w1_ref[0, 0:half, :], preferred_element_type=F32)
    q = jnp.dot(xb, w1_ref[0, half:2 * half, :], preferred_element_type=F32)
    n_sub = x.shape[0]
    h = p + pltpu.roll(q, n_sub - 1, 0)
    h = jax.nn.gelu(h, approximate=True)
    o_ref[0, 0] = jnp.dot(h.astype(BF16), w2_ref[0], preferred_element_type=F32).astype(BF16)


def _compress(cmp_raw, batch, seq, pe_k, w1_k, w2_k, pe_v, w1_v, w2_v):
    n_sub = seq // CMP_STRIDE
    flat = CMP_STRIDE * HEAD_DIM
    xr = cmp_raw.reshape(batch, n_sub, CMP_STRIDE, 2 * NSA_KV_HEADS, HEAD_DIM)
    xr = xr.transpose(0, 3, 1, 2, 4).reshape(batch, 2 * NSA_KV_HEADS, n_sub, flat)
    pe = jnp.stack([pe_k.reshape(2, flat), pe_v.reshape(2, flat)])
    w1 = jnp.stack([w1_k, w1_v]).astype(BF16)
    w2 = jnp.stack([w2_k, w2_v]).astype(BF16)
    w2 = jnp.concatenate([w2, w2], axis=-1)
    return pl.pallas_call(
        _compress_kernel,
        out_shape=jax.ShapeDtypeStruct((batch, 2 * NSA_KV_HEADS, n_sub, LANES), BF16),
        grid=(batch, 2 * NSA_KV_HEADS),
        in_specs=[
            pl.BlockSpec((1, 1, n_sub, flat), lambda b, j: (b, j, 0, 0)),
            pl.BlockSpec((1, 2, flat), lambda b, j: (j % 2, 0, 0)),
            pl.BlockSpec((1, CMP_BLOCK * HEAD_DIM, CMP_HIDDEN), lambda b, j: (j % 2, 0, 0)),
            pl.BlockSpec((1, CMP_HIDDEN, LANES), lambda b, j: (j % 2, 0, 0)),
        ],
        out_specs=pl.BlockSpec((1, 1, n_sub, LANES), lambda b, j: (b, j, 0, 0)),
        compiler_params=_cparams(2),
        name="compress",
    )(xr, pe, w1, w2)


def _scores(q, k):
    return lax.dot_general(q, k, (((1,), (1,)), ((), ())), preferred_element_type=F32)


def _split_heads(pair):
    lane = lax.broadcasted_iota(jnp.int32, pair.shape, 1)
    zero = jnp.zeros_like(pair)
    return jnp.where(lane < HEAD_DIM, pair, zero), jnp.where(lane >= HEAD_DIM, pair, zero)


def _softmax_pv(parts):
    m = None
    for s, _ in parts:
        mi = jnp.max(s, axis=-1, keepdims=True)
        m = mi if m is None else jnp.maximum(m, mi)
    l = None
    acc = None
    for s, v in parts:
        p = jnp.exp(s - m)
        li = jnp.sum(p, axis=-1, keepdims=True)
        ai = jnp.dot(p.astype(BF16), v, preferred_element_type=F32)
        l = li if l is None else l + li
        acc = ai if acc is None else acc + ai
    return acc / l


def _diff_kernel(q_ref, k_ref, v_ref, lam_ref, g_ref, o_ref):
    tq = DIFF_TQ
    seq = q_ref.shape[1]
    lp = lam_ref[...]
    lam = (jnp.exp(jnp.sum(lp[0:1] * lp[1:2], axis=-1, keepdims=True))
           - jnp.exp(jnp.sum(lp[2:3] * lp[3:4], axis=-1, keepdims=True)) + LAMBDA_INIT)
    row = lax.broadcasted_iota(jnp.int32, (2 * tq, tq), 0) & (tq - 1)
    col = lax.broadcasted_iota(jnp.int32, (2 * tq, tq), 1)
    causal = col <= row
    for i in range(seq // tq):
        lo = i * tq
        q1, q2 = _split_heads(q_ref[0, lo:lo + tq, :])
        qq = jnp.concatenate([q1, q2], axis=0)
        parts = []
        if i > 0:
            parts.append((_scores(qq, k_ref[0, 0:lo, :]), v_ref[0, 0:lo, :]))
        s_diag = jnp.where(causal, _scores(qq, k_ref[0, lo:lo + tq, :]), NEG_INF)
        parts.append((s_diag, v_ref[0, lo:lo + tq, :]))
        o = _softmax_pv(parts)
        o = o[0:tq] - lam * o[tq:2 * tq]
        o_ref[0, lo:lo + tq, :] = (_rms(o, g_ref[...]) * (1.0 - LAMBDA_INIT)).astype(BF16)


def _diff_attention(y, lam_params, subln, batch, seq):
    return pl.pallas_call(
        _diff_kernel,
        out_shape=jax.ShapeDtypeStruct((batch, seq, DIFF_HEADS * 2 * HEAD_DIM), BF16),
        grid=(batch, DIFF_HEADS),
        in_specs=[
            pl.BlockSpec((1, seq, LANES), lambda b, h: (b, 0, h)),
            pl.BlockSpec((1, seq, LANES), lambda b, h: (b, 0, DIFF_HEADS + h)),
            pl.BlockSpec((1, seq, LANES), lambda b, h: (b, 0, 2 * DIFF_HEADS + h)),
            pl.BlockSpec((4, HEAD_DIM), lambda b, h: (0, 0)),
            pl.BlockSpec((1, LANES), lambda b, h: (0, 0)),
        ],
        out_specs=pl.BlockSpec((1, seq, LANES), lambda b, h: (b, 0, h)),
        compiler_params=_cparams(2),
        name="diffattn",
    )(y, y, y, lam_params, subln.reshape(1, -1))


def _block_onehot(seq):
    e = (np.arange(seq)[:, None] // SEL_BLOCK) == np.arange(LANES)[None, :]
    return jnp.asarray(e, dtype=BF16)


def _overlap_t(seq):
    n_sel = seq // SEL_BLOCK
    cs = np.arange(LANES)[None, :] * CMP_STRIDE
    ss = np.arange(n_sel)[:, None] * SEL_BLOCK
    ov = np.minimum(cs + CMP_BLOCK, ss + SEL_BLOCK) - np.maximum(cs, ss)
    ov = np.clip(ov, 0, None) / CMP_BLOCK
    ov[:, seq // CMP_STRIDE - CMP_BLOCK // CMP_STRIDE + 1:] = 0.0
    return jnp.asarray(ov, dtype=BF16)


def _nsa_kernel(q_ref, ks_ref, kw_ref, vs_ref, vw_ref, kc_ref, vc_ref, gate_ref, e_ref, ov_ref, o_ref, ke_ref):
    seq = q_ref.shape[1]
    ke_ref[:, 0:LANES] = ks_ref[0]
    ke_ref[:, LANES:2 * LANES] = e_ref[...]
    for i in range(seq // ATT_TQ):
        _nsa_tile(i, q_ref, ke_ref, kw_ref, vs_ref, vw_ref, kc_ref, vc_ref, gate_ref, ov_ref, o_ref)


def _nsa_tile(qi, q_ref, ke_ref, kw_ref, vs_ref, vw_ref, kc_ref, vc_ref, gate_ref, ov_ref, o_ref):
    tq = ATT_TQ
    hg = NSA_GROUP
    n_sel = ov_ref.shape[0]
    lo = qi * tq
    q = q_ref[0, lo:lo + tq, :]
    heads = _split_heads(q[:, 0:LANES]) + _split_heads(q[:, LANES:2 * LANES])
    q4 = jnp.concatenate(heads, axis=0)
    row = lax.broadcasted_iota(jnp.int32, (hg * tq, tq), 0) & (tq - 1)
    col = lax.broadcasted_iota(jnp.int32, (hg * tq, tq), 1)

    n_cmp = kc_ref.shape[2]
    t_row = qi * tq + (lax.broadcasted_iota(jnp.int32, (hg * tq, n_cmp), 0) & (tq - 1))
    c_end = lax.broadcasted_iota(jnp.int32, (hg * tq, n_cmp), 1) * CMP_STRIDE + (CMP_BLOCK - 1)
    c_valid = c_end <= t_row
    s_c = jnp.where(c_valid, _scores(q4, kc_ref[0, 0]), NEG_INF)
    p_c = jnp.exp(s_c - jnp.max(s_c, axis=-1, keepdims=True))
    p_c = jnp.where(c_valid, p_c / jnp.sum(p_c, axis=-1, keepdims=True), 0.0)
    o_cmp = jnp.dot(p_c.astype(BF16), vc_ref[0, 0], preferred_element_type=F32)

    p_sum = p_c[0:tq]
    for h in range(1, hg):
        p_sum = p_sum + p_c[h * tq:(h + 1) * tq]
    p_hi = p_sum.astype(BF16)
    p_lo = (p_sum - p_hi.astype(F32)).astype(BF16)
    ov = ov_ref[...]
    imp = _scores(ov, p_hi) + _scores(ov, p_lo)
    blk = lax.broadcasted_iota(jnp.int32, (n_sel, tq), 0)
    cur = jnp.right_shift(qi * tq + lax.broadcasted_iota(jnp.int32, (n_sel, tq), 1), SEL_SHIFT)
    forced = (blk == 0) | ((blk <= cur) & (blk >= cur - 1))
    imp = jnp.where(forced, FORCE_SCORE, jnp.where(blk <= cur, imp, -1.0))
    rank = jnp.zeros((n_sel, tq), F32)
    for k in range(n_sel):
        r = imp[k:k + 1, :]
        rank = rank + jnp.where(blk > k, jnp.where(r >= imp, 1.0, 0.0), jnp.where(r > imp, 1.0, 0.0))
    pen = jnp.where(rank < min(SEL_TOP_N, n_sel), 0.0, NEG_INF)
    pen = jnp.concatenate([pen, jnp.zeros((LANES - n_sel, tq), F32)], axis=0)
    pen = jnp.transpose(pen).astype(BF16)
    q4e = jnp.concatenate([q4, jnp.concatenate([pen] * hg, axis=0)], axis=1)

    parts = []
    if qi > 0:
        parts.append((_scores(q4e, ke_ref[0:lo, :]), vs_ref[0, 0:lo, :]))
    s = jnp.where(col <= row, _scores(q4e, ke_ref[lo:lo + tq, :]), NEG_INF)
    parts.append((s, vs_ref[0, lo:lo + tq, :]))
    o_sel = _softmax_pv(parts)

    parts = []
    if qi >= 2:
        s = jnp.where(col > row, _scores(q4, kw_ref[0, lo - 2 * tq:lo - tq, :]), NEG_INF)
        parts.append((s, vw_ref[0, lo - 2 * tq:lo - tq, :]))
    if qi >= 1:
        parts.append((_scores(q4, kw_ref[0, lo - tq:lo, :]), vw_ref[0, lo - tq:lo, :]))
    s = jnp.where(col <= row, _scores(q4, kw_ref[0, lo:lo + tq, :]), NEG_INF)
    parts.append((s, vw_ref[0, lo:lo + tq, :]))
    o_win = _softmax_pv(parts)

    gate = gate_ref[0, lo:lo + tq, :]
    outs = []
    for h in range(hg):
        rs = slice(h * tq, (h + 1) * tq)
        c = h * N_BRANCH
        outs.append(gate[:, c:c + 1] * o_cmp[rs] + gate[:, c + 1:c + 2] * o_sel[rs]
                    + gate[:, c + 2:c + 3] * o_win[rs])
    lane = lax.broadcasted_iota(jnp.int32, (tq, LANES), 1)
    pairs = [jnp.where(lane < HEAD_DIM, outs[2 * j], outs[2 * j + 1]) for j in range(hg // 2)]
    o_ref[0, lo:lo + tq, :] = jnp.concatenate(pairs, axis=1).astype(BF16)


def _nsa_attention(y, kcvc, gates, batch, seq):
    n_sub = seq // CMP_STRIDE
    n_sel = seq // SEL_BLOCK
    assert n_sub == LANES and n_sel <= LANES
    kv0 = 16
    return pl.pallas_call(
        _nsa_kernel,
        out_shape=jax.ShapeDtypeStruct((batch, seq, NSA_HEADS * HEAD_DIM), BF16),
        grid=(batch, NSA_KV_HEADS),
        in_specs=[
            pl.BlockSpec((1, seq, 2 * LANES), lambda b, g: (b, 0, 6 + g)),
            pl.BlockSpec((1, seq, LANES), lambda b, g: (b, 0, kv0 + 2 * g)),
            pl.BlockSpec((1, seq, LANES), lambda b, g: (b, 0, kv0 + 2 * g + 1)),
            pl.BlockSpec((1, seq, LANES), lambda b, g: (b, 0, kv0 + 4 + 2 * g)),
            pl.BlockSpec((1, seq, LANES), lambda b, g: (b, 0, kv0 + 4 + 2 * g + 1)),
            pl.BlockSpec((1, 1, n_sub, LANES), lambda b, g: (b, 2 * g, 0, 0)),
            pl.BlockSpec((1, 1, n_sub, LANES), lambda b, g: (b, 2 * g + 1, 0, 0)),
            pl.BlockSpec((1, seq, LANES), lambda b, g: (b, 0, g)),
            pl.BlockSpec((seq, LANES), lambda b, g: (0, 0)),
            pl.BlockSpec((n_sel, LANES), lambda b, g: (0, 0)),
        ],
        out_specs=pl.BlockSpec((1, seq, 2 * LANES), lambda b, g: (b, 0, g)),
        scratch_shapes=[pltpu.VMEM((seq, 2 * LANES), BF16)],
        compiler_params=_cparams(2),
        name="nsa",
    )(y, y, y, y, y, kcvc, kcvc, gates, _block_onehot(seq), _overlap_t(seq))


def _outproj_kernel(od_ref, on_ref, w_ref, g_ref, x_ref, o_ref):
    half = od_ref.shape[1]
    h = jnp.dot(od_ref[...], w_ref[0:half, :], preferred_element_type=F32)
    h = h + jnp.dot(on_ref[...], w_ref[half:2 * half, :], preferred_element_type=F32)
    o_ref[...] = x_ref[...] + _rms(h, g_ref[...])


def _outproj(o_diff, o_nsa, w_out, g, x):
    t = x.shape[0]
    half = o_diff.shape[1]
    return pl.pallas_call(
        _outproj_kernel,
        out_shape=jax.ShapeDtypeStruct((t, D_MODEL), F32),
        grid=(t // PROJ_TM,),
        in_specs=[
            pl.BlockSpec((PROJ_TM, half), lambda i: (i, 0)),
            pl.BlockSpec((PROJ_TM, half), lambda i: (i, 0)),
            _const_spec((2 * half, D_MODEL)),
            _const_spec((1, D_MODEL)),
            pl.BlockSpec((PROJ_TM, D_MODEL), lambda i: (i, 0)),
        ],
        out_specs=pl.BlockSpec((PROJ_TM, D_MODEL), lambda i: (i, 0)),
        compiler_params=_cparams(1),
        name="outproj",
    )(o_diff, o_nsa, w_out.astype(BF16), g.reshape(1, -1), x)


def kernel(x, ff1_norm_pre, ff1_w_gate, ff1_w_up, ff1_w_down, ff1_norm_post, mix_norm_pre, w_in, lambda_q1, lambda_k1, lambda_q2, lambda_k2, diff_subln, cmp_pe_k, cmp_k_w1, cmp_k_w2, cmp_pe_v, cmp_v_w1, cmp_v_w2, w_out, mix_norm_post, ff2_norm_pre, ff2_w_gate, ff2_w_up, ff2_w_down, ff2_norm_post):
    batch, seq, d = x.shape
    assert d == D_MODEL and ff1_w_gate.shape[0] == 1, "single-layer kernel"
    t = batch * seq
    xf = x.reshape(t, d)

    x1 = _ffn(xf, ff1_norm_pre[0], ff1_w_gate[0], ff1_w_up[0], ff1_w_down[0], ff1_norm_post[0])

    y, cmp_raw, gates = _inproj(x1, mix_norm_pre[0], w_in[0], seq)
    kcvc = _compress(cmp_raw, batch, seq, cmp_pe_k[0], cmp_k_w1[0], cmp_k_w2[0],
                     cmp_pe_v[0], cmp_v_w1[0], cmp_v_w2[0])
    y = y.reshape(batch, seq, Y_WIDTH)
    lam_params = jnp.stack([lambda_q1[0], lambda_k1[0], lambda_q2[0], lambda_k2[0]]).astype(F32)
    o_diff = _diff_attention(y, lam_params, diff_subln[0], batch, seq)
    o_nsa = _nsa_attention(y, kcvc, gates.reshape(batch, seq, 256), batch, seq)

    x2 = _outproj(o_diff.reshape(t, -1), o_nsa.reshape(t, -1), w_out[0], mix_norm_post[0], x1)
    out = _ffn(x2, ff2_norm_pre[0], ff2_w_gate[0], ff2_w_up[0], ff2_w_down[0], ff2_norm_post[0])
    return out.reshape(batch, seq, d)
```

```python
import functools
import math

import numpy as np
import jax
import jax.numpy as jnp
from jax import lax
from jax.experimental import pallas as pl
from jax.experimental.pallas import tpu as pltpu

F32 = jnp.float32
BF16 = jnp.bfloat16

D_MODEL = 1024
HEAD_DIM = 64
ROPE_DIM = HEAD_DIM // 4
ROPE_THETA = 500000.0
DIFF_HEADS = 4
NSA_HEADS = 8
NSA_KV_HEADS = 2
NSA_GROUP = NSA_HEADS // NSA_KV_HEADS
N_BRANCH = 3
CMP_BLOCK = 32
CMP_STRIDE = 16
CMP_HIDDEN = 4 * HEAD_DIM
SEL_BLOCK = 64
SEL_SHIFT = SEL_BLOCK.bit_length() - 1
assert 1 << SEL_SHIFT == SEL_BLOCK
SEL_TOP_N = 16
WINDOW = 512
D_FF = 2816
EPS = 1e-6
NEG_INF = -1e30
FORCE_SCORE = 1e9
LAMBDA_INIT = 0.8 - 0.6 * math.exp(-0.3 * 0)
Q_SCALE = HEAD_DIM ** -0.5 * math.log2(math.e)

LANES = 128
VMEM_LIMIT = 56 * 1024 * 1024

Y_WIDTH = 3072
CMP_OFF = Y_WIDTH
GATE_OFF = CMP_OFF + 256
W_IN_WIDTH = GATE_OFF + 256
ROPE_GROUPS = (0, 1, 3, 4)
SCALE_GROUPS = (0, 3)

FFN_TM = 512
FFN_FC = 256
PROJ_TM = 512
ATT_TQ = 256
assert WINDOW == 2 * ATT_TQ
DIFF_TQ = 256
ONES_ROWS = 16


def _rms(x, g):
    return x * lax.rsqrt(jnp.mean(x * x, axis=-1, keepdims=True) + EPS) * g


def _cparams(n_axes):
    return pltpu.CompilerParams(dimension_semantics=("arbitrary",) * n_axes,
                                vmem_limit_bytes=VMEM_LIMIT)


def _const_spec(shape):
    return pl.BlockSpec(shape, lambda *_: (0,) * len(shape), pipeline_mode=pl.Buffered(1))


def _swiglu_block(x, gpre_ref, wg_ref, wu_ref, wd_ref, gpost_ref, a_ref):
    h = _rms(x, gpre_ref[...]).astype(BF16)
    for c in range(D_FF // FFN_FC):
        cs = slice(c * FFN_FC, (c + 1) * FFN_FC)
        g = jnp.dot(h, wg_ref[:, cs], preferred_element_type=F32)
        u = jnp.dot(h, wu_ref[:, cs], preferred_element_type=F32)
        a_ref[:, cs] = (g * jax.nn.sigmoid(g) * u).astype(BF16)
    y = jnp.dot(a_ref[...], wd_ref[...], preferred_element_type=F32)
    return x + 0.5 * _rms(y, gpost_ref[...])


def _ffn_kernel(x_ref, gpre_ref, wg_ref, wu_ref, wd_ref, gpost_ref, o_ref, a_ref):
    o_ref[...] = _swiglu_block(x_ref[...], gpre_ref, wg_ref, wu_ref, wd_ref, gpost_ref, a_ref)


def _mix_ffn_kernel(od_ref, on_ref, wo_ref, gmix_ref, x_ref, gpre_ref, wg_ref, wu_ref, wd_ref, gpost_ref,
                    o_ref, a_ref):
    half = od_ref.shape[1]
    h = jnp.dot(od_ref[...], wo_ref[0:half, :], preferred_element_type=F32)
    h = h + jnp.dot(on_ref[...], wo_ref[half:2 * half, :], preferred_element_type=F32)
    x = x_ref[...] + _rms(h, gmix_ref[...])
    o_ref[...] = _swiglu_block(x, gpre_ref, wg_ref, wu_ref, wd_ref, gpost_ref, a_ref)


def _ffn_specs():
    return [_const_spec((1, D_MODEL)), _const_spec((D_MODEL, D_FF)), _const_spec((D_MODEL, D_FF)),
            _const_spec((D_FF, D_MODEL)), _const_spec((1, D_MODEL))]


def _ffn_args(g_pre, w_gate, w_up, w_down, g_post):
    return (g_pre.reshape(1, -1), w_gate.astype(BF16), w_up.astype(BF16), w_down.astype(BF16),
            g_post.reshape(1, -1))


def _ffn(x, g_pre, w_gate, w_up, w_down, g_post):
    t = x.shape[0]
    tile = pl.BlockSpec((FFN_TM, D_MODEL), lambda i: (i, 0))
    return pl.pallas_call(
        _ffn_kernel,
        out_shape=jax.ShapeDtypeStruct((t, D_MODEL), F32),
        grid=(t // FFN_TM,),
        in_specs=[tile] + _ffn_specs(),
        out_specs=tile,
        scratch_shapes=[pltpu.VMEM((FFN_TM, D_FF), BF16)],
        compiler_params=_cparams(1),
        name="ffn",
    )(x, *_ffn_args(g_pre, w_gate, w_up, w_down, g_post))


def _mix_ffn(o_diff, o_nsa, w_out, g_mix, x, g_pre, w_gate, w_up, w_down, g_post):
    t = x.shape[0]
    half = o_diff.shape[1]
    tile = pl.BlockSpec((FFN_TM, D_MODEL), lambda i: (i, 0))
    head = pl.BlockSpec((FFN_TM, half), lambda i: (i, 0))
    return pl.pallas_call(
        _mix_ffn_kernel,
        out_shape=jax.ShapeDtypeStruct((t, D_MODEL), F32),
        grid=(t // FFN_TM,),
        in_specs=[head, head, _const_spec((2 * half, D_MODEL)), _const_spec((1, D_MODEL)), tile] + _ffn_specs(),
        out_specs=tile,
        scratch_shapes=[pltpu.VMEM((FFN_TM, D_FF), BF16)],
        compiler_params=_cparams(1),
        name="mixffn",
    )(o_diff, o_nsa, w_out.astype(BF16), g_mix.reshape(1, -1), x, *_ffn_args(g_pre, w_gate, w_up, w_down, g_post))


def _w_in_columns():
    dq, dk, dv, nq, kc, vc, ks, vs, kw, vw, gl = np.cumsum(
        [0, 512, 512, 512, 512, 128, 128, 128, 128, 128, 128])
    cols = list(range(dq, kc))

    def dup(base, g):
        c = list(range(base + g * HEAD_DIM, base + (g + 1) * HEAD_DIM))
        return c + c

    for g in range(NSA_KV_HEADS):
        cols += dup(ks, g) + dup(kw, g)
    for g in range(NSA_KV_HEADS):
        cols += dup(vs, g) + dup(vw, g)
    for g in range(NSA_KV_HEADS):
        cols += list(range(kc + g * HEAD_DIM, kc + (g + 1) * HEAD_DIM))
        cols += list(range(vc + g * HEAD_DIM, vc + (g + 1) * HEAD_DIM))
    per_group = NSA_GROUP * N_BRANCH
    for g in range(NSA_KV_HEADS):
        cols += list(range(gl + g * per_group, gl + (g + 1) * per_group)) + [-1] * (LANES - per_group)
    assert len(cols) == W_IN_WIDTH
    return np.asarray(cols)


def _rope_tables(seq):
    pos = jnp.arange(seq, dtype=F32)
    inv = ROPE_THETA ** (-jnp.arange(0, ROPE_DIM, 2, dtype=F32) / ROPE_DIM)
    ang = pos[:, None] * inv[None, :]
    cos, sin = jnp.cos(ang), jnp.sin(ang)
    half = ROPE_DIM // 2
    lane = np.arange(LANES) % HEAD_DIM
    idx = lane % half
    c = jnp.where(lane < ROPE_DIM, cos[:, idx], 1.0)
    sa = jnp.where(lane < half, -sin[:, idx], 0.0)
    sb = jnp.where((lane >= half) & (lane < ROPE_DIM), sin[:, idx], 0.0)
    return jnp.concatenate([c, sa, sb], axis=1)


def _inproj_kernel(x_ref, g_ref, w_ref, rope_ref, y_ref, cmp_ref, gate_ref):
    h = _rms(x_ref[...], g_ref[...]).astype(BF16)
    cos = rope_ref[:, 0:LANES]
    sin_a = rope_ref[:, LANES:2 * LANES]
    sin_b = rope_ref[:, 2 * LANES:3 * LANES]
    half = ROPE_DIM // 2
    for grp in range(Y_WIDTH // 512):
        z = jnp.dot(h, w_ref[:, grp * 512:(grp + 1) * 512], preferred_element_type=F32)
        for s in range(4):
            zz = z[:, s * LANES:(s + 1) * LANES]
            if grp in ROPE_GROUPS:
                zz = zz * cos + pltpu.roll(zz, LANES - half, 1) * sin_a + pltpu.roll(zz, half, 1) * sin_b
            if grp in SCALE_GROUPS:
                zz = zz * Q_SCALE
            lo = grp * 512 + s * LANES
            y_ref[:, lo:lo + LANES] = zz.astype(BF16)
    cmp_ref[...] = jnp.dot(h, w_ref[:, CMP_OFF:CMP_OFF + 256], preferred_element_type=F32)
    gate_ref[...] = jax.nn.sigmoid(jnp.dot(h, w_ref[:, GATE_OFF:GATE_OFF + 256], preferred_element_type=F32))


def _inproj(x, g, w_in, seq):
    t = x.shape[0]
    cols = _w_in_columns()
    w = jnp.where(cols[None, :] >= 0, w_in[:, np.maximum(cols, 0)], 0.0).astype(BF16)
    rope = _rope_tables(seq)
    n_pos = seq // PROJ_TM
    return pl.pallas_call(
        _inproj_kernel,
        out_shape=(jax.ShapeDtypeStruct((t, Y_WIDTH), BF16),
                   jax.ShapeDtypeStruct((t, 256), F32),
                   jax.ShapeDtypeStruct((t, 256), F32)),
        grid=(t // PROJ_TM,),
        in_specs=[
            pl.BlockSpec((PROJ_TM, D_MODEL), lambda i: (i, 0)),
            _const_spec((1, D_MODEL)),
            _const_spec((D_MODEL, W_IN_WIDTH)),
            pl.BlockSpec((PROJ_TM, 3 * LANES), lambda i: (i % n_pos, 0)),
        ],
        out_specs=(pl.BlockSpec((PROJ_TM, Y_WIDTH), lambda i: (i, 0)),
                   pl.BlockSpec((PROJ_TM, 256), lambda i: (i, 0)),
                   pl.BlockSpec((PROJ_TM, 256), lambda i: (i, 0))),
        compiler_params=_cparams(1),
        name="inproj",
    )(x, g.reshape(1, -1), w, rope)


def _compress_kernel(x_ref, pe_ref, w1_ref, w2_ref, o_ref):
    x = x_ref[0, 0]
    half = CMP_STRIDE * HEAD_DIM
    xa = (x + pe_ref[0, 0:1, :]).astype(BF16)
    xb = (x + pe_ref[0, 1:2, :]).astype(BF16)
    p = jnp.dot(xa, w1_ref[0, 0:half, :], preferred_element_type=F32)
    q = jnp.dot(xb, w1_ref[0, half:2 * half, :], preferred_element_type=F32)
    n_sub = x.shape[0]
    h = p + pltpu.roll(q, n_sub - 1, 0)
    h = jax.nn.gelu(h, approximate=True)
    o_ref[0, 0] = jnp.dot(h.astype(BF16), w2_ref[0], preferred_element_type=F32).astype(BF16)


def _compress(cmp_raw, batch, seq, pe_k, w1_k, w2_k, pe_v, w1_v, w2_v):
    n_sub = seq // CMP_STRIDE
    flat = CMP_STRIDE * HEAD_DIM
    xr = cmp_raw.reshape(batch, n_sub, CMP_STRIDE, 2 * NSA_KV_HEADS, HEAD_DIM)
    xr = xr.transpose(0, 3, 1, 2, 4).reshape(batch, 2 * NSA_KV_HEADS, n_sub, flat)
    pe = jnp.stack([pe_k.reshape(2, flat), pe_v.reshape(2, flat)])
    w1 = jnp.stack([w1_k, w1_v]).astype(BF16)
    w2 = jnp.stack([w2_k, w2_v]).astype(BF16)
    w2 = jnp.concatenate([w2, w2], axis=-1)
    return pl.pallas_call(
        _compress_kernel,
        out_shape=jax.ShapeDtypeStruct((batch, 2 * NSA_KV_HEADS, n_sub, LANES), BF16),
        grid=(batch, 2 * NSA_KV_HEADS),
        in_specs=[
            pl.BlockSpec((1, 1, n_sub, flat), lambda b, j: (b, j, 0, 0)),
            pl.BlockSpec((1, 2, flat), lambda b, j: (j % 2, 0, 0)),
            pl.BlockSpec((1, CMP_BLOCK * HEAD_DIM, CMP_HIDDEN), lambda b, j: (j % 2, 0, 0)),
            pl.BlockSpec((1, CMP_HIDDEN, LANES), lambda b, j: (j % 2, 0, 0)),
        ],
        out_specs=pl.BlockSpec((1, 1, n_sub, LANES), lambda b, j: (b, j, 0, 0)),
        compiler_params=_cparams(2),
        name="compress",
    )(xr, pe, w1, w2)


def _scores(a, b):
    return lax.dot_general(a, b, (((1,), (1,)), ((), ())), preferred_element_type=F32)


def _split_heads(pair):
    lane = lax.broadcasted_iota(jnp.int32, pair.shape, 1)
    zero = jnp.zeros_like(pair)
    return jnp.where(lane < HEAD_DIM, pair, zero), jnp.where(lane >= HEAD_DIM, pair, zero)


def _vt_with_ones(v, dv):
    vt = jnp.transpose(v.astype(F32))[0:dv]
    return jnp.concatenate([vt, jnp.ones((ONES_ROWS, v.shape[0]), F32)], axis=0).astype(BF16)


def _softmax_pv_t(pieces, vt):
    m = None
    for s in pieces:
        mi = jnp.max(s, axis=0, keepdims=True)
        m = mi if m is None else jnp.maximum(m, mi)
    p = [jnp.exp2(s - m).astype(BF16) for s in pieces]
    p = p[0] if len(p) == 1 else jnp.concatenate(p, axis=0)
    return jnp.dot(vt, p, preferred_element_type=F32)


def _diff_kernel(q_ref, k_ref, v_ref, lam_ref, g_ref, o_ref, vt_ref):
    tq = DIFF_TQ
    seq = q_ref.shape[1]
    dv = 2 * HEAD_DIM
    vt_ref[...] = _vt_with_ones(v_ref[0], dv)
    lp = lam_ref[...]
    lam = (jnp.exp(jnp.sum(lp[0:1] * lp[1:2], axis=-1, keepdims=True))
           - jnp.exp(jnp.sum(lp[2:3] * lp[3:4], axis=-1, keepdims=True)) + LAMBDA_INIT)
    key = lax.broadcasted_iota(jnp.int32, (tq, 2 * tq), 0)
    qry = lax.broadcasted_iota(jnp.int32, (tq, 2 * tq), 1) & (tq - 1)
    causal = key <= qry

    def qk(i):
        lo = i * tq
        q1, q2 = _split_heads(q_ref[0, lo:lo + tq, :])
        qq = jnp.concatenate([q1, q2], axis=0)
        pieces = []
        if i > 0:
            pieces.append(_scores(k_ref[0, 0:lo, :], qq))
        pieces.append(jnp.where(causal, _scores(k_ref[0, lo:lo + tq, :], qq), NEG_INF))
        return pieces

    order = list(range(seq // tq))[::-1]
    nxt = qk(order[0])
    for pos, i in enumerate(order):
        lo = i * tq
        pieces = nxt
        if pos + 1 < len(order):
            nxt = qk(order[pos + 1])
        o = _softmax_pv_t(pieces, vt_ref[:, 0:lo + tq])
        o = o[0:dv] / o[dv:dv + 1]
        o = jnp.transpose(o[:, 0:tq] - lam * o[:, tq:2 * tq])
        o_ref[0, lo:lo + tq, :] = (_rms(o, g_ref[...]) * (1.0 - LAMBDA_INIT)).astype(BF16)


def _diff_attention(y, lam_params, subln, batch, seq):
    return pl.pallas_call(
        _diff_kernel,
        out_shape=jax.ShapeDtypeStruct((batch, seq, DIFF_HEADS * 2 * HEAD_DIM), BF16),
        grid=(batch, DIFF_HEADS),
        in_specs=[
            pl.BlockSpec((1, seq, LANES), lambda b, h: (b, 0, h)),
            pl.BlockSpec((1, seq, LANES), lambda b, h: (b, 0, DIFF_HEADS + h)),
            pl.BlockSpec((1, seq, LANES), lambda b, h: (b, 0, 2 * DIFF_HEADS + h)),
            pl.BlockSpec((4, HEAD_DIM), lambda b, h: (0, 0)),
            pl.BlockSpec((1, LANES), lambda b, h: (0, 0)),
        ],
        out_specs=pl.BlockSpec((1, seq, LANES), lambda b, h: (b, 0, h)),
        scratch_shapes=[pltpu.VMEM((2 * HEAD_DIM + ONES_ROWS, seq), BF16)],
        compiler_params=_cparams(2),
        name="diffattn",
    )(y, y, y, lam_params, subln.reshape(1, -1))


def _block_onehot(seq):
    e = (np.arange(seq)[:, None] // SEL_BLOCK) == np.arange(LANES)[None, :]
    return jnp.asarray(e, dtype=BF16)


def _overlap_t(seq):
    n_sel = seq // SEL_BLOCK
    cs = np.arange(LANES)[None, :] * CMP_STRIDE
    ss = np.arange(n_sel)[:, None] * SEL_BLOCK
    ov = np.minimum(cs + CMP_BLOCK, ss + SEL_BLOCK) - np.maximum(cs, ss)
    ov = np.clip(ov, 0, None) / CMP_BLOCK
    ov[:, seq // CMP_STRIDE - CMP_BLOCK // CMP_STRIDE + 1:] = 0.0
    return jnp.asarray(ov, dtype=BF16)


def _nsa_kernel(q_ref, ks_ref, kw_ref, vs_ref, vw_ref, kc_ref, vc_ref, gate_ref, e_ref, ov_ref, o_ref,
                ke_ref, vst_ref, vwt_ref, vct_ref):
    seq = q_ref.shape[1]
    ke_ref[:, 0:LANES] = ks_ref[0]
    ke_ref[:, LANES:2 * LANES] = e_ref[...]
    vst_ref[...] = _vt_with_ones(vs_ref[0], HEAD_DIM)
    vwt_ref[...] = _vt_with_ones(vw_ref[0], HEAD_DIM)
    vct_ref[...] = _vt_with_ones(vc_ref[0, 0], HEAD_DIM)
    order = list(range(seq // ATT_TQ))[::-1]
    refs = dict(q=q_ref, ke=ke_ref, kw=kw_ref, vst=vst_ref, vwt=vwt_ref, kc=kc_ref, vct=vct_ref,
                gate=gate_ref, ov=ov_ref, o=o_ref)
    sel = {i: _nsa_select(i, refs) for i in order[:2]}
    sc = {order[0]: _nsa_scores(order[0], sel[order[0]], refs)}
    for pos, i in enumerate(order):
        if pos + 1 < len(order):
            sc[order[pos + 1]] = _nsa_scores(order[pos + 1], sel[order[pos + 1]], refs)
        _nsa_finish(i, sel.pop(i), sc.pop(i), refs)
        if pos + 2 < len(order):
            sel[order[pos + 2]] = _nsa_select(order[pos + 2], refs)


def _nsa_select(qi, r):
    tq = ATT_TQ
    hg = NSA_GROUP
    dv = HEAD_DIM
    n_sel = r["ov"].shape[0]
    lo = qi * tq
    q = r["q"][0, lo:lo + tq, :]
    heads = _split_heads(q[:, 0:LANES]) + _split_heads(q[:, LANES:2 * LANES])
    q4 = jnp.concatenate(heads, axis=0)

    n_cmp = r["kc"].shape[2]
    t_q = lo + (lax.broadcasted_iota(jnp.int32, (n_cmp, hg * tq), 1) & (tq - 1))
    c_end = lax.broadcasted_iota(jnp.int32, (n_cmp, hg * tq), 0) * CMP_STRIDE + (CMP_BLOCK - 1)
    c_valid = c_end <= t_q
    s_c = jnp.where(c_valid, _scores(r["kc"][0, 0], q4), NEG_INF)
    p_c = jnp.exp2(s_c - jnp.max(s_c, axis=0, keepdims=True))
    p_c = jnp.where(c_valid, p_c / jnp.sum(p_c, axis=0, keepdims=True), 0.0)
    o_cmp = jnp.dot(r["vct"][0:dv, :], p_c.astype(BF16), preferred_element_type=F32)

    p_sum = p_c[:, 0:tq]
    for h in range(1, hg):
        p_sum = p_sum + p_c[:, h * tq:(h + 1) * tq]
    p_hi = p_sum.astype(BF16)
    p_lo = (p_sum - p_hi.astype(F32)).astype(BF16)
    ov = r["ov"][...]
    imp = jnp.dot(ov, p_hi, preferred_element_type=F32) + jnp.dot(ov, p_lo, preferred_element_type=F32)
    blk = lax.broadcasted_iota(jnp.int32, (n_sel, tq), 0)
    cur = jnp.right_shift(lo + lax.broadcasted_iota(jnp.int32, (n_sel, tq), 1), SEL_SHIFT)
    forced = (blk == 0) | ((blk <= cur) & (blk >= cur - 1))
    imp = jnp.where(forced, FORCE_SCORE, jnp.where(blk <= cur, imp, -1.0))
    rank = jnp.zeros((n_sel, tq), F32)
    for k in range(n_sel):
        rk = imp[k:k + 1, :]
        rank = rank + jnp.where(blk > k, jnp.where(rk >= imp, 1.0, 0.0), jnp.where(rk > imp, 1.0, 0.0))
    pen = jnp.where(rank < min(SEL_TOP_N, n_sel), 0.0, NEG_INF)
    pen = jnp.concatenate([pen, jnp.zeros((LANES - n_sel, tq), F32)], axis=0)
    pen = jnp.transpose(pen).astype(BF16)
    q4e = jnp.concatenate([q4, jnp.concatenate([pen] * hg, axis=0)], axis=1)
    return q4, q4e, o_cmp


def _nsa_scores(qi, sel, r):
    tq = ATT_TQ
    q4, q4e, _ = sel
    lo = qi * tq
    key = lax.broadcasted_iota(jnp.int32, (tq, NSA_GROUP * tq), 0)
    qry = lax.broadcasted_iota(jnp.int32, (tq, NSA_GROUP * tq), 1) & (tq - 1)
    s_sel = []
    if qi > 0:
        s_sel.append(_scores(r["ke"][0:lo, :], q4e))
    s_sel.append(jnp.where(key <= qry, _scores(r["ke"][lo:lo + tq, :], q4e), NEG_INF))
    s_win = []
    if qi >= 2:
        s_win.append(jnp.where(key > qry, _scores(r["kw"][0, lo - 2 * tq:lo - tq, :], q4), NEG_INF))
    if qi >= 1:
        s_win.append(_scores(r["kw"][0, lo - tq:lo, :], q4))
    s_win.append(jnp.where(key <= qry, _scores(r["kw"][0, lo:lo + tq, :], q4), NEG_INF))
    return s_sel, s_win


def _nsa_finish(qi, sel, sc, r):
    tq = ATT_TQ
    dv = HEAD_DIM
    lo = qi * tq
    o_cmp = sel[2]
    s_sel, s_win = sc
    o_sel = _softmax_pv_t(s_sel, r["vst"][:, 0:lo + tq])
    o_sel = o_sel[0:dv] / o_sel[dv:dv + 1]
    o_win = _softmax_pv_t(s_win, r["vwt"][:, max(qi - 2, 0) * tq:lo + tq])
    o_win = o_win[0:dv] / o_win[dv:dv + 1]
    gate = jnp.transpose(r["gate"][0, lo:lo + tq, :])
    outs = []
    for h in range(NSA_GROUP):
        cs = slice(h * tq, (h + 1) * tq)
        c = h * N_BRANCH
        outs.append(gate[c:c + 1] * o_cmp[:, cs] + gate[c + 1:c + 2] * o_sel[:, cs]
                    + gate[c + 2:c + 3] * o_win[:, cs])
    r["o"][0, lo:lo + tq, :] = jnp.transpose(jnp.concatenate(outs, axis=0)).astype(BF16)


def _nsa_attention(y, kcvc, gates, batch, seq):
    n_sub = seq // CMP_STRIDE
    n_sel = seq // SEL_BLOCK
    assert n_sub == LANES and n_sel <= LANES
    kv0 = 16
    return pl.pallas_call(
        _nsa_kernel,
        out_shape=jax.ShapeDtypeStruct((batch, seq, NSA_HEADS * HEAD_DIM), BF16),
        grid=(batch, NSA_KV_HEADS),
        in_specs=[
            pl.BlockSpec((1, seq, 2 * LANES), lambda b, g: (b, 0, 6 + g)),
            pl.BlockSpec((1, seq, LANES), lambda b, g: (b, 0, kv0 + 2 * g)),
            pl.BlockSpec((1, seq, LANES), lambda b, g: (b, 0, kv0 + 2 * g + 1)),
            pl.BlockSpec((1, seq, LANES), lambda b, g: (b, 0, kv0 + 4 + 2 * g)),
            pl.BlockSpec((1, seq, LANES), lambda b, g: (b, 0, kv0 + 4 + 2 * g + 1)),
            pl.BlockSpec((1, 1, n_sub, LANES), lambda b, g: (b, 2 * g, 0, 0)),
            pl.BlockSpec((1, 1, n_sub, LANES), lambda b, g: (b, 2 * g + 1, 0, 0)),
            pl.BlockSpec((1, seq, LANES), lambda b, g: (b, 0, g)),
            pl.BlockSpec((seq, LANES), lambda b, g: (0, 0)),
            pl.BlockSpec((n_sel, LANES), lambda b, g: (0, 0)),
        ],
        out_specs=pl.BlockSpec((1, seq, 2 * LANES), lambda b, g: (b, 0, g)),
        scratch_shapes=[pltpu.VMEM((seq, 2 * LANES), BF16),
                        pltpu.VMEM((HEAD_DIM + ONES_ROWS, seq), BF16),
                        pltpu.VMEM((HEAD_DIM + ONES_ROWS, seq), BF16),
                        pltpu.VMEM((HEAD_DIM + ONES_ROWS, n_sub), BF16)],
        compiler_params=_cparams(2),
        name="nsa",
    )(y, y, y, y, y, kcvc, kcvc, gates, _block_onehot(seq), _overlap_t(seq))


def kernel(x, ff1_norm_pre, ff1_w_gate, ff1_w_up, ff1_w_down, ff1_norm_post, mix_norm_pre, w_in, lambda_q1, lambda_k1, lambda_q2, lambda_k2, diff_subln, cmp_pe_k, cmp_k_w1, cmp_k_w2, cmp_pe_v, cmp_v_w1, cmp_v_w2, w_out, mix_norm_post, ff2_norm_pre, ff2_w_gate, ff2_w_up, ff2_w_down, ff2_norm_post):
    batch, seq, d = x.shape
    assert d == D_MODEL and ff1_w_gate.shape[0] == 1, "single-layer kernel"
    t = batch * seq
    xf = x.reshape(t, d)

    x1 = _ffn(xf, ff1_norm_pre[0], ff1_w_gate[0], ff1_w_up[0], ff1_w_down[0], ff1_norm_post[0])

    y, cmp_raw, gates = _inproj(x1, mix_norm_pre[0], w_in[0], seq)
    kcvc = _compress(cmp_raw, batch, seq, cmp_pe_k[0], cmp_k_w1[0], cmp_k_w2[0],
                     cmp_pe_v[0], cmp_v_w1[0], cmp_v_w2[0])
    y = y.reshape(batch, seq, Y_WIDTH)
    lam_params = jnp.stack([lambda_q1[0], lambda_k1[0], lambda_q2[0], lambda_k2[0]]).astype(F32)
    o_diff = _diff_attention(y, lam_params, diff_subln[0], batch, seq)
    o_nsa = _nsa_attention(y, kcvc, gates.reshape(batch, seq, 256), batch, seq)

    out = _mix_ffn(o_diff.reshape(t, -1), o_nsa.reshape(t, -1), w_out[0], mix_norm_post[0], x1,
                   ff2_norm_pre[0], ff2_w_gate[0], ff2_w_up[0], ff2_w_down[0], ff2_norm_post[0])
    return out.reshape(batch, seq, d)
```
